```python
import math
import jax, jax.numpy as jnp
from jax import lax
import numpy as np

D_MODEL = 1024
BATCH = 8
SEQ = 2048
DEPTH = 4
DEC_BATCH = 128
DEC_SEQ = 4
PAST_LEN = 8192
PAGE_SIZE = 128

MIX_WIDTH = D_MODEL
GDN_HEADS = 4
GDN_HEAD_DIM = (MIX_WIDTH // 2) // GDN_HEADS
GDN_KEY = GDN_HEADS * GDN_HEAD_DIM
GDN_VAL = GDN_HEADS * GDN_HEAD_DIM
CONV_WIDTH = 4
CONV_CH = 2 * GDN_KEY + GDN_VAL
GDN_CHUNK = 64
SWA_HEADS = 8
SWA_KV_HEADS = 2
SWA_WIDTH = MIX_WIDTH - GDN_VAL
SWA_HEAD_DIM = SWA_WIDTH // SWA_HEADS
SWA_GROUP = SWA_HEADS // SWA_KV_HEADS
SWA_KV_WIDTH = SWA_KV_HEADS * SWA_HEAD_DIM
WINDOW = 128
D_FF = 2816
EPS = 1e-6
L2_EPS = 1e-6
OFF_Z = CONV_CH
OFF_B = OFF_Z + GDN_VAL
OFF_A = OFF_B + GDN_HEADS
OFF_SQ = OFF_A + GDN_HEADS
OFF_SK = OFF_SQ + SWA_WIDTH
OFF_SV = OFF_SK + SWA_KV_WIDTH
IN_COLS = OFF_SV + SWA_KV_WIDTH

kernel_name = "hymba_gdn_swa_sink_macaron_step"


def _rmsnorm(x, w):
    xf = x.astype(jnp.float32)
    y = xf * lax.rsqrt(jnp.mean(xf * xf, axis=-1, keepdims=True) + EPS)
    return (y * w.astype(jnp.float32)).astype(x.dtype)


def _swiglu(x, w_gate, w_up, w_down):
    return (jax.nn.silu(x @ w_gate) * (x @ w_up)) @ w_down


def _l2norm(x):
    return x * lax.rsqrt(jnp.sum(x * x, axis=-1, keepdims=True) + L2_EPS)


def _short_conv(u, buf, w):
    L = u.shape[1]
    full = jnp.concatenate([buf.astype(u.dtype), u], axis=1)
    out = sum(full[:, i:i + L] * w[i] for i in range(CONV_WIDTH))
    return jax.nn.silu(out), full[:, -(CONV_WIDTH - 1):]


def _gated_delta_chunked(q, k, v, g, beta, s0):
    B, L, H, DK = q.shape
    DV = v.shape[-1]
    C = min(GDN_CHUNK, L)
    pad = (-L) % C
    if pad:
        pw = ((0, 0), (0, pad), (0, 0), (0, 0))
        q, k, v = jnp.pad(q, pw), jnp.pad(k, pw), jnp.pad(v, pw)
        g, beta = jnp.pad(g, pw[:3]), jnp.pad(beta, pw[:3])
    N = (L + pad) // C

    def blk(t):
        return jnp.moveaxis(t.reshape((B, N, C, H) + t.shape[3:]), 3, 1)

    q, k, v, g, beta = blk(q), blk(k), blk(v), blk(g), blk(beta)
    gc = jnp.cumsum(g, axis=-1)
    idx = jnp.arange(C)
    causal = idx[:, None] >= idx[None, :]
    strict = idx[:, None] > idx[None, :]
    decay = jnp.exp(jnp.where(causal, gc[..., :, None] - gc[..., None, :], -jnp.inf))
    kb = k * beta[..., None]
    a_mat = jnp.where(strict, jnp.einsum('bhnid,bhnjd->bhnij', kb, k) * decay, 0.0)
    lhs = a_mat + jnp.eye(C, dtype=jnp.float32)
    rhs = jnp.concatenate([v * beta[..., None], kb * jnp.exp(gc)[..., None]], axis=-1)
    sol = lax.linalg.triangular_solve(lhs, rhs, left_side=True, lower=True, unit_diagonal=True)
    w_val, k_cum = sol[..., :DV], sol[..., DV:]
    qk = jnp.einsum('bhnid,bhnjd->bhnij', q, k) * decay
    q_dec = q * jnp.exp(gc)[..., None]
    k_dec = k * jnp.exp(gc[..., -1:] - gc)[..., None]
    c_dec = jnp.exp(gc[..., -1])
    xs = tuple(jnp.moveaxis(t, 2, 0) for t in (w_val, k_cum, qk, q_dec, k_dec, c_dec))

    def step(S, inp):
        w_i, kc_i, qk_i, qd_i, kd_i, cd_i = inp
        u = w_i - jnp.einsum('bhck,bhkv->bhcv', kc_i, S)
        o = jnp.einsum('bhck,bhkv->bhcv', qd_i, S) + jnp.einsum('bhij,bhjv->bhiv', qk_i, u)
        S = S * cd_i[..., None, None] + jnp.einsum('bhck,bhcv->bhkv', kd_i, u)
        return S, o

    s_final, o = lax.scan(step, s0, xs)
    o = jnp.moveaxis(jnp.moveaxis(o, 0, 2), 1, 3).reshape(B, N * C, H, DV)[:, :L]
    return o, s_final


def _gdn_branch(h_qkv, z, b, a, conv_buf, s0, conv_w, a_log, dt_bias, norm_w):
    B, L, _ = h_qkv.shape
    u, new_buf = _short_conv(h_qkv, conv_buf, conv_w)
    u = u.astype(jnp.float32)
    q = _l2norm(u[..., :GDN_KEY].reshape(B, L, GDN_HEADS, GDN_HEAD_DIM)) * (GDN_HEAD_DIM ** -0.5)
    k = _l2norm(u[..., GDN_KEY:2 * GDN_KEY].reshape(B, L, GDN_HEADS, GDN_HEAD_DIM))
    v = u[..., 2 * GDN_KEY:].reshape(B, L, GDN_HEADS, GDN_HEAD_DIM)
    beta = jax.nn.sigmoid(b.astype(jnp.float32))
    g = -jnp.exp(a_log.astype(jnp.float32)) * jax.nn.softplus(a.astype(jnp.float32) + dt_bias.astype(jnp.float32))
    o, s_new = _gated_delta_chunked(q, k, v, g, beta, s0.astype(jnp.float32))
    o = o * lax.rsqrt(jnp.mean(o * o, axis=-1, keepdims=True) + EPS) * norm_w.astype(jnp.float32)
    o = o * jax.nn.silu(z.astype(jnp.float32).reshape(B, L, GDN_HEADS, GDN_HEAD_DIM))
    return o.reshape(B, L, GDN_VAL).astype(h_qkv.dtype), new_buf, s_new


def _sink_attention(q, k, v, allowed, sinks):
    s = jnp.einsum('bnqhgd,bnkhd->bnhgqk', q, k).astype(jnp.float32) * (SWA_HEAD_DIM ** -0.5)
    s = jnp.where(allowed[None, :, None, None], s, -jnp.inf)
    sink = jnp.broadcast_to(sinks.astype(jnp.float32)[None, None, :, :, None, None], s.shape[:-1] + (1,))
    p = jax.nn.softmax(jnp.concatenate([s, sink], axis=-1), axis=-1)[..., :-1]
    return jnp.einsum('bnhgqk,bnkhd->bnqhgd', p.astype(v.dtype), v)


def _swa_branch(q, k, v, k_buf, v_buf, sinks, norm_w):
    B, L = q.shape[:2]
    q = q.reshape(B, L, SWA_KV_HEADS, SWA_GROUP, SWA_HEAD_DIM)
    k = k.reshape(B, L, SWA_KV_HEADS, SWA_HEAD_DIM)
    v = v.reshape(B, L, SWA_KV_HEADS, SWA_HEAD_DIM)
    sk = sinks.reshape(SWA_KV_HEADS, SWA_GROUP)
    if k_buf is None:
        N = L // WINDOW
        qb = q.reshape(B, N, WINDOW, SWA_KV_HEADS, SWA_GROUP, SWA_HEAD_DIM)

        def band(t):
            prev = jnp.concatenate([jnp.zeros_like(t[:, :WINDOW]), t[:, :L - WINDOW]], axis=1)
            shp = (B, N, WINDOW, SWA_KV_HEADS, SWA_HEAD_DIM)
            return jnp.concatenate([prev.reshape(shp), t.reshape(shp)], axis=2)

        i = jnp.arange(WINDOW)[:, None]
        j = jnp.arange(2 * WINDOW)[None, :]
        diff = i + WINDOW - j
        local = (diff >= 0) & (diff < WINDOW)
        allowed = local[None] & ((jnp.arange(N) > 0)[:, None, None] | (j >= WINDOW)[None])
        o = _sink_attention(qb, band(k), band(v), allowed, sk).reshape(B, L, SWA_WIDTH)
        buf_len = min(WINDOW, L)
        k_new, v_new = k[:, -buf_len:], v[:, -buf_len:]
    else:
        buf_len = k_buf.shape[1]
        kk = jnp.concatenate([k_buf.astype(k.dtype), k], axis=1)
        vv = jnp.concatenate([v_buf.astype(v.dtype), v], axis=1)
        i = jnp.arange(L)[:, None]
        j = jnp.arange(buf_len + L)[None, :]
        diff = buf_len + i - j
        allowed = ((diff >= 0) & (diff < WINDOW))[None]
        o = _sink_attention(q[:, None], kk[:, None], vv[:, None], allowed, sk).reshape(B, L, SWA_WIDTH)
        k_new, v_new = kk[:, -buf_len:], vv[:, -buf_len:]
    return _rmsnorm(o, norm_w), k_new, v_new


def _layer(x, conv_buf, gdn_s, k_buf, v_buf, p):
    (n1, g1, u1, d1, nm, w_in, conv_w, a_log, dt_bias, gdn_norm, sinks, swa_norm, w_out, n2, g2, u2, d2) = p
    x = x + 0.5 * _swiglu(_rmsnorm(x, n1), g1, u1, d1)
    h = _rmsnorm(x, nm) @ w_in
    gdn_o, conv_new, s_new = _gdn_branch(h[..., :OFF_Z], h[..., OFF_Z:OFF_B], h[..., OFF_B:OFF_A],
                                         h[..., OFF_A:OFF_SQ], conv_buf, gdn_s, conv_w, a_log, dt_bias, gdn_norm)
    swa_o, k_new, v_new = _swa_branch(h[..., OFF_SQ:OFF_SK], h[..., OFF_SK:OFF_SV], h[..., OFF_SV:],
                                      k_buf, v_buf, sinks, swa_norm)
    x = x + jnp.concatenate([gdn_o, swa_o], axis=-1) @ w_out
    x = x + 0.5 * _swiglu(_rmsnorm(x, n2), g2, u2, d2)
    return x, conv_new, s_new, k_new, v_new


def setup_inputs(seed: int = 0) -> dict:
    key = jax.random.key(seed)
    ks = jax.random.split(key, 32)
    f32 = jnp.float32
    nrm = lambda k, shp, s: jax.random.normal(k, shp, f32) * s
    gain = lambda k, shp: 1.0 + 0.02 * jax.random.normal(k, shp, f32)
    buf_len = min(WINDOW, PAST_LEN)
    a_init = jax.random.uniform(ks[10], (DEPTH, GDN_HEADS), f32, 1.0, 16.0)
    dt = jnp.exp(jax.random.uniform(ks[11], (DEPTH, GDN_HEADS), f32) * (math.log(0.1) - math.log(0.001)) + math.log(0.001))
    return {
        "x_prompt": nrm(ks[0], (BATCH, SEQ, D_MODEL), 1.0),
        "x_sample": nrm(ks[1], (DEC_BATCH, DEC_SEQ, D_MODEL), 1.0),
        "state_gdn_conv": nrm(ks[2], (DEPTH, DEC_BATCH, CONV_WIDTH - 1, CONV_CH), 1.0),
        "state_gdn": nrm(ks[3], (DEPTH, DEC_BATCH, GDN_HEADS, GDN_HEAD_DIM, GDN_HEAD_DIM), GDN_HEAD_DIM ** -0.5),
        "cache_swa_k": nrm(ks[4], (DEPTH, DEC_BATCH, buf_len, SWA_KV_HEADS, SWA_HEAD_DIM), 1.0),
        "cache_swa_v": nrm(ks[5], (DEPTH, DEC_BATCH, buf_len, SWA_KV_HEADS, SWA_HEAD_DIM), 1.0),
        "ffn1_norm": gain(ks[6], (DEPTH, D_MODEL)),
        "ffn1_w_gate": nrm(ks[7], (DEPTH, D_MODEL, D_FF), D_MODEL ** -0.5),
        "ffn1_w_up": nrm(ks[8], (DEPTH, D_MODEL, D_FF), D_MODEL ** -0.5),
        "ffn1_w_down": nrm(ks[9], (DEPTH, D_FF, D_MODEL), D_FF ** -0.5),
        "mix_norm": gain(ks[12], (DEPTH, D_MODEL)),
        "w_in": nrm(ks[13], (DEPTH, D_MODEL, IN_COLS), D_MODEL ** -0.5),
        "gdn_conv_w": nrm(ks[14], (DEPTH, CONV_WIDTH, CONV_CH), CONV_WIDTH ** -0.5),
        "gdn_a_log": jnp.log(a_init),
        "gdn_dt_bias": dt + jnp.log(-jnp.expm1(-dt)),
        "gdn_out_norm": gain(ks[15], (DEPTH, GDN_HEAD_DIM)),
        "swa_sinks": nrm(ks[16], (DEPTH, SWA_HEADS), 0.5),
        "swa_out_norm": gain(ks[17], (DEPTH, SWA_WIDTH)),
        "w_out": nrm(ks[18], (DEPTH, MIX_WIDTH, D_MODEL), MIX_WIDTH ** -0.5),
        "ffn2_norm": gain(ks[19], (DEPTH, D_MODEL)),
        "ffn2_w_gate": nrm(ks[20], (DEPTH, D_MODEL, D_FF), D_MODEL ** -0.5),
        "ffn2_w_up": nrm(ks[21], (DEPTH, D_MODEL, D_FF), D_MODEL ** -0.5),
        "ffn2_w_down": nrm(ks[22], (DEPTH, D_FF, D_MODEL), D_FF ** -0.5),
        "final_norm": gain(ks[23], (D_MODEL,)),
    }


def reference(x_prompt, x_sample, state_gdn_conv, state_gdn, cache_swa_k, cache_swa_v,
              ffn1_norm, ffn1_w_gate, ffn1_w_up, ffn1_w_down, mix_norm, w_in, gdn_conv_w,
              gdn_a_log, gdn_dt_bias, gdn_out_norm, swa_sinks, swa_out_norm, w_out,
              ffn2_norm, ffn2_w_gate, ffn2_w_up, ffn2_w_down, final_norm):
    xp, xs = x_prompt, x_sample
    bp = xp.shape[0]
    pc, ps, pk, pv = [], [], [], []
    sc, ss, sk, sv = [], [], [], []
    for l in range(DEPTH):
        p = (ffn1_norm[l], ffn1_w_gate[l], ffn1_w_up[l], ffn1_w_down[l], mix_norm[l], w_in[l],
             gdn_conv_w[l], gdn_a_log[l], gdn_dt_bias[l], gdn_out_norm[l], swa_sinks[l],
             swa_out_norm[l], w_out[l], ffn2_norm[l], ffn2_w_gate[l], ffn2_w_up[l], ffn2_w_down[l])
        xp, c, s, kn, vn = _layer(xp, jnp.zeros((bp, CONV_WIDTH - 1, CONV_CH), xp.dtype),
                                  jnp.zeros((bp, GDN_HEADS, GDN_HEAD_DIM, GDN_HEAD_DIM), jnp.float32),
                                  None, None, p)
        pc.append(c); ps.append(s.astype(state_gdn.dtype)); pk.append(kn); pv.append(vn)
        xs, c, s, kn, vn = _layer(xs, state_gdn_conv[l], state_gdn[l], cache_swa_k[l], cache_swa_v[l], p)
        sc.append(c); ss.append(s.astype(state_gdn.dtype)); sk.append(kn); sv.append(vn)
    y_prompt = _rmsnorm(xp, final_norm)
    y_sample = _rmsnorm(xs, final_norm)
    return (y_prompt, y_sample,
            jnp.stack(pc), jnp.stack(ps), jnp.stack(pk), jnp.stack(pv),
            jnp.stack(sc), jnp.stack(ss), jnp.stack(sk), jnp.stack(sv))
```

```python
import functools

import jax
import jax.numpy as jnp
from jax import lax
from jax.experimental import pallas as pl
from jax.experimental.pallas import tpu as pltpu

F32 = jnp.float32
BF16 = jnp.bfloat16

D_MODEL = 1024
D_FF = 2816
GDN_HEADS = 4
HEAD_DIM = 128
GDN_KEY = GDN_HEADS * HEAD_DIM
CONV_CH = 3 * GDN_KEY
CONV_WIDTH = 4
SWA_HEADS = 8
SWA_KV_HEADS = 2
SWA_HEAD_DIM = 64
SWA_WIDTH = SWA_HEADS * SWA_HEAD_DIM
SWA_KV_WIDTH = SWA_KV_HEADS * SWA_HEAD_DIM
WINDOW = 128
EPS = 1e-6
L2_EPS = 1e-6
NEG = -1e30

LANES = 128
TILE = 128
SEQ_PAD = 8
NEW_OFF = 4
DENSE_TM = 256
VMEM_LIMIT = 56 * 1024 * 1024

H_QKV = 0
H_Z = CONV_CH
H_SQ = H_Z + GDN_KEY
H_SK = H_SQ + SWA_WIDTH
H_SV = H_SK + SWA_KV_WIDTH
H_B = H_SV + SWA_KV_WIDTH
H_A = H_B + LANES
H_COLS = H_A + LANES


def _sigmoid(x):
    return 1.0 / (1.0 + jnp.exp(-x))


def _silu(x):
    return x * _sigmoid(x)


def _softplus(x):
    return jnp.maximum(x, 0.0) + jnp.log(1.0 + jnp.exp(-jnp.abs(x)))


def _rms(x, w):
    return x * lax.rsqrt(jnp.mean(x * x, axis=-1, keepdims=True) + EPS) * w


def _dot(a, b):
    return jnp.dot(a, b, preferred_element_type=F32)


def _dot_nt(a, b):
    return lax.dot_general(a, b, (((1,), (1,)), ((), ())), preferred_element_type=F32)


def _dot_tn(a, b):
    return lax.dot_general(a, b, (((0,), (0,)), ((), ())), preferred_element_type=F32)


def _dot_hi(a, b):
    return jnp.dot(a, b, preferred_element_type=F32, precision=lax.Precision.HIGHEST)


def _swiglu_half(x, nw, wg_ref, wu_ref, wd_ref):
    xn = _rms(x, nw).astype(BF16)
    g = _dot(xn, wg_ref[...])
    u = _dot(xn, wu_ref[...])
    act = (_silu(g) * u).astype(BF16)
    return x + 0.5 * _dot(act, wd_ref[...])


def _dense_in_kernel(x_ref, n1_ref, wg_ref, wu_ref, wd_ref, nm_ref, win_ref, wat_ref,
                     xo_ref, h_ref, at_ref):
    x = _swiglu_half(x_ref[...], n1_ref[...], wg_ref, wu_ref, wd_ref)
    xo_ref[...] = x
    xn = _rms(x, nm_ref[...]).astype(BF16)
    h_ref[...] = _dot(xn, win_ref[...])
    at_ref[...] = _dot_nt(wat_ref[...], xn)


def _dense_out_kernel(x_ref, go_ref, so_ref, wog_ref, wos_ref, n2_ref, wg_ref, wu_ref, wd_ref,
                      fn_ref, o_ref, *, final):
    x = x_ref[...] + _dot(go_ref[...].astype(BF16), wog_ref[...]) + _dot(so_ref[...].astype(BF16), wos_ref[...])
    x = _swiglu_half(x, n2_ref[...], wg_ref, wu_ref, wd_ref)
    o_ref[...] = _rms(x, fn_ref[...]) if final else x


def _const_spec(shape):
    return pl.BlockSpec(shape, lambda *_: (0,) * len(shape), pipeline_mode=pl.Buffered(1))


def _dense_in(x, n1, wg, wu, wd, nm, win, wat):
    t = x.shape[0]
    row = lambda w: pl.BlockSpec((DENSE_TM, w), lambda i: (i, 0))
    return pl.pallas_call(
        _dense_in_kernel,
        grid=(t // DENSE_TM,),
        in_specs=[row(D_MODEL), _const_spec((1, D_MODEL)), _const_spec((D_MODEL, D_FF)),
                  _const_spec((D_MODEL, D_FF)), _const_spec((D_FF, D_MODEL)),
                  _const_spec((1, D_MODEL)), _const_spec((D_MODEL, H_COLS)),
                  _const_spec((SEQ_PAD, D_MODEL))],
        out_specs=[row(D_MODEL), row(H_COLS), pl.BlockSpec((SEQ_PAD, DENSE_TM), lambda i: (0, i))],
        out_shape=[jax.ShapeDtypeStruct((t, D_MODEL), F32),
                   jax.ShapeDtypeStruct((t, H_COLS), F32),
                   jax.ShapeDtypeStruct((SEQ_PAD, t), F32)],
        compiler_params=pltpu.CompilerParams(dimension_semantics=("arbitrary",),
                                             vmem_limit_bytes=VMEM_LIMIT),
        name="dense_in",
    )(x, n1, wg, wu, wd, nm, win, wat)


def _dense_out(x, go, so, wog, wos, n2, wg, wu, wd, fn, final):
    t = x.shape[0]
    row = lambda w: pl.BlockSpec((DENSE_TM, w), lambda i: (i, 0))
    return pl.pallas_call(
        functools.partial(_dense_out_kernel, final=final),
        grid=(t // DENSE_TM,),
        in_specs=[row(D_MODEL), row(GDN_KEY), row(SWA_WIDTH),
                  _const_spec((GDN_KEY, D_MODEL)), _const_spec((SWA_WIDTH, D_MODEL)),
                  _const_spec((1, D_MODEL)), _const_spec((D_MODEL, D_FF)),
                  _const_spec((D_MODEL, D_FF)), _const_spec((D_FF, D_MODEL)),
                  _const_spec((1, D_MODEL))],
        out_specs=row(D_MODEL),
        out_shape=jax.ShapeDtypeStruct((t, D_MODEL), F32),
        compiler_params=pltpu.CompilerParams(dimension_semantics=("arbitrary",),
                                             vmem_limit_bytes=VMEM_LIMIT),
        name="dense_out",
    )(x, go, so, wog, wos, n2, wg, wu, wd, fn)


def _unit_lower_inverse(a, ri, ci, seq_rows):
    eye = (ri == ci).astype(F32)
    lvl = ((ri >> 1) == (ci >> 1)) & ((ri & 1) == 1) & ((ci & 1) == 0)
    d = eye - jnp.where(lvl, a, 0.0)
    s = 2
    while s < seq_rows:
        sh = s.bit_length() - 1
        lvl = ((ri >> (sh + 1)) == (ci >> (sh + 1))) & (((ri >> sh) & 1) == 1) & (((ci >> sh) & 1) == 0)
        l = jnp.where(lvl, a, 0.0).astype(BF16)
        db = d.astype(BF16)
        d = d - _dot(db, _dot(l, db).astype(BF16))
        s *= 2
    return d


def _gdn_kernel(*refs, seq_rows):
    sample = seq_rows < TILE
    if sample:
        (qkv_ref, z_ref, b_ref, a_ref, at_ref, cs_ref, sin_ref, cw_ref, alc_ref, dtc_ref, alr_ref,
         dtr_ref, gnw_ref, go_ref, sout_ref, xbuf, k_scr, q_scr, ks_scr, qs_scr, kd_scr, u_scr,
         c_scr) = refs
    else:
        (qkv_ref, z_ref, b_ref, a_ref, at_ref, cw_ref, alc_ref, dtc_ref, alr_ref,
         dtr_ref, gnw_ref, go_ref, sout_ref, xbuf) = refs
    n_seq = TILE // seq_rows
    sh = seq_rows.bit_length() - 1

    ri = lax.broadcasted_iota(jnp.int32, (TILE, TILE), 0)
    ci = lax.broadcasted_iota(jnp.int32, (TILE, TILE), 1)
    same = (ri >> sh) == (ci >> sh)
    causal = same & (ri >= ci)
    strict = same & (ri > ci)

    cur = qkv_ref[...]
    if sample:
        rq = lax.broadcasted_iota(jnp.int32, (TILE, CONV_CH), 0) & (SEQ_PAD - 1)
        cur = jnp.where(rq >= NEW_OFF, cur, cs_ref[...])
        xbuf[0:SEQ_PAD, :] = jnp.zeros((SEQ_PAD, CONV_CH), F32)
    else:
        @pl.when(pl.program_id(1) == 0)
        def _():
            xbuf[0:SEQ_PAD, :] = jnp.zeros((SEQ_PAD, CONV_CH), F32)
            sout_ref[...] = jnp.zeros(sout_ref.shape, F32)
    xbuf[SEQ_PAD:SEQ_PAD + TILE, :] = cur

    def conv(c0):
        acc = None
        for i in range(CONV_WIDTH):
            r0 = SEQ_PAD - (CONV_WIDTH - 1) + i
            term = xbuf[r0:r0 + TILE, c0:c0 + LANES] * cw_ref[i:i + 1, c0:c0 + LANES]
            acc = term if acc is None else acc + term
        return _silu(acc)

    beta_c = _sigmoid(b_ref[...])
    g_c = -jnp.exp(alc_ref[...]) * _softplus(a_ref[...] + dtc_ref[...])
    g_r = -jnp.exp(alr_ref[...]) * _softplus(at_ref[...] + dtr_ref[...])
    if sample:
        beta_c = jnp.where((ri & (SEQ_PAD - 1)) >= NEW_OFF, beta_c, 0.0)
        g_c = jnp.where((ri & (SEQ_PAD - 1)) >= NEW_OFF, g_c, 0.0)
        cr = lax.broadcasted_iota(jnp.int32, (SEQ_PAD, TILE), 1) & (SEQ_PAD - 1)
        g_r = jnp.where(cr >= NEW_OFF, g_r, 0.0)
    gc_c = _dot_hi(causal.astype(F32), g_c)
    gl_c = _dot_hi(same.astype(F32), g_c)
    gc_r = _dot_hi(g_r, (same & (ri <= ci)).astype(F32))

    heads = []
    for h in range(GDN_HEADS):
        q = conv(h * HEAD_DIM)
        k = conv(GDN_KEY + h * HEAD_DIM)
        v = conv(2 * GDN_KEY + h * HEAD_DIM)
        q = q * lax.rsqrt(jnp.sum(q * q, axis=-1, keepdims=True) + L2_EPS) * (HEAD_DIM ** -0.5)
        k = k * lax.rsqrt(jnp.sum(k * k, axis=-1, keepdims=True) + L2_EPS)
        qb, kb = q.astype(BF16), k.astype(BF16)
        gci = gc_c[:, h:h + 1]
        decay = jnp.exp(jnp.where(causal, gci - gc_r[h:h + 1, :], NEG))
        beta = beta_c[:, h:h + 1]
        a_mat = jnp.where(strict, beta * _dot_nt(kb, kb) * decay, 0.0)
        t_inv = _unit_lower_inverse(a_mat, ri, ci, seq_rows)
        qkd = _dot_nt(qb, kb) * decay
        egc = jnp.exp(gci)
        kdec = k * jnp.exp(gl_c[:, h:h + 1] - gci)
        heads.append((q, k, v, beta, egc, t_inv, qkd, kdec))
        if sample:
            k_scr[h] = k
            q_scr[h] = q
            c_scr[h] = jnp.broadcast_to(jnp.exp(gl_c[:, h:h + 1]), (TILE, LANES))

    if sample:
        def load_state(s, _):
            r = pl.multiple_of(s * SEQ_PAD, SEQ_PAD)
            for h in range(GDN_HEADS):
                st = sin_ref[s, h]
                ks_scr[h, pl.ds(r, SEQ_PAD), :] = _dot(k_scr[h, pl.ds(r, SEQ_PAD), :], st)
                qs_scr[h, pl.ds(r, SEQ_PAD), :] = _dot(q_scr[h, pl.ds(r, SEQ_PAD), :], st)
            return 0
        lax.fori_loop(0, n_seq, load_state, 0)

    for h in range(GDN_HEADS):
        q, k, v, beta, egc, t_inv, qkd, kdec = heads[h]
        if sample:
            k_s, q_s = ks_scr[h], qs_scr[h]
        else:
            st = sout_ref[0, h].astype(BF16)
            k_s, q_s = _dot(k.astype(BF16), st), _dot(q.astype(BF16), st)
        rhs = beta * (v - egc * k_s)
        u = _dot(t_inv.astype(BF16), rhs.astype(BF16))
        ub = u.astype(BF16)
        o = egc * q_s + _dot(qkd.astype(BF16), ub)
        o = o * lax.rsqrt(jnp.mean(o * o, axis=-1, keepdims=True) + EPS) * gnw_ref[...]
        o = o * _silu(z_ref[:, h * HEAD_DIM:(h + 1) * HEAD_DIM])
        go_ref[:, h * HEAD_DIM:(h + 1) * HEAD_DIM] = o.astype(go_ref.dtype)
        if sample:
            kd_scr[h] = kdec
            u_scr[h] = u
        else:
            c = jnp.exp(gl_c[0:1, h:h + 1])
            sout_ref[0, h] = sout_ref[0, h] * c + _dot_tn(kdec.astype(BF16), ub)

    if sample:
        def store_state(s, _):
            r = pl.multiple_of(s * SEQ_PAD, SEQ_PAD)
            for h in range(GDN_HEADS):
                c = c_scr[h, pl.ds(r + SEQ_PAD - 1, 1), :]
                upd = _dot_tn(kd_scr[h, pl.ds(r, SEQ_PAD), :], u_scr[h, pl.ds(r, SEQ_PAD), :])
                sout_ref[s, h] = sin_ref[s, h] * c + upd
            return 0
        lax.fori_loop(0, n_seq, store_state, 0)
    else:
        xbuf[0:SEQ_PAD, :] = xbuf[TILE:TILE + SEQ_PAD, :]


def _gdn_param_specs():
    return [_const_spec((CONV_WIDTH, CONV_CH)), _const_spec((1, LANES)), _const_spec((1, LANES)),
            _const_spec((SEQ_PAD, LANES)), _const_spec((SEQ_PAD, LANES)), _const_spec((1, HEAD_DIM))]


def _gdn_prompt(h, at, params, batch, seq):
    nt = seq // TILE
    rb = lambda b, n: b * nt + n
    hspec = lambda w, off: pl.BlockSpec((TILE, w), lambda b, n: (rb(b, n), off // w))
    return pl.pallas_call(
        functools.partial(_gdn_kernel, seq_rows=TILE),
        grid=(batch, nt),
        in_specs=[hspec(CONV_CH, H_QKV), hspec(GDN_KEY, H_Z), hspec(LANES, H_B), hspec(LANES, H_A),
                  pl.BlockSpec((SEQ_PAD, TILE), lambda b, n: (0, rb(b, n)))] + _gdn_param_specs(),
        out_specs=[pl.BlockSpec((TILE, GDN_KEY), lambda b, n: (rb(b, n), 0)),
                   pl.BlockSpec((1, GDN_HEADS, HEAD_DIM, HEAD_DIM), lambda b, n: (b, 0, 0, 0))],
        out_shape=[jax.ShapeDtypeStruct((batch * seq, GDN_KEY), BF16),
                   jax.ShapeDtypeStruct((batch, GDN_HEADS, HEAD_DIM, HEAD_DIM), F32)],
        scratch_shapes=[pltpu.VMEM((TILE + SEQ_PAD, CONV_CH), F32)],
        compiler_params=pltpu.CompilerParams(dimension_semantics=("arbitrary", "arbitrary"),
                                             vmem_limit_bytes=VMEM_LIMIT),
        name="gdn_prompt",
    )(h, h, h, h, at, *params)


def _gdn_sample(h, at, conv_state, state, params):
    n_seq = TILE // SEQ_PAD
    nt = h.shape[0] // TILE
    hspec = lambda w, off: pl.BlockSpec((TILE, w), lambda i: (i, off // w))
    sspec = pl.BlockSpec((n_seq, GDN_HEADS, HEAD_DIM, HEAD_DIM), lambda i: (i, 0, 0, 0))
    hscr = pltpu.VMEM((GDN_HEADS, TILE, HEAD_DIM), F32)
    return pl.pallas_call(
        functools.partial(_gdn_kernel, seq_rows=SEQ_PAD),
        grid=(nt,),
        in_specs=[hspec(CONV_CH, H_QKV), hspec(GDN_KEY, H_Z), hspec(LANES, H_B), hspec(LANES, H_A),
                  pl.BlockSpec((SEQ_PAD, TILE), lambda i: (0, i)),
                  pl.BlockSpec((TILE, CONV_CH), lambda i: (i, 0)), sspec] + _gdn_param_specs(),
        out_specs=[pl.BlockSpec((TILE, GDN_KEY), lambda i: (i, 0)), sspec],
        out_shape=[jax.ShapeDtypeStruct((h.shape[0], GDN_KEY), BF16),
                   jax.ShapeDtypeStruct(state.shape, F32)],
        scratch_shapes=[pltpu.VMEM((TILE + SEQ_PAD, CONV_CH), F32)] + [hscr] * 7,
        compiler_params=pltpu.CompilerParams(dimension_semantics=("arbitrary",),
                                             vmem_limit_bytes=VMEM_LIMIT),
        name="gdn_sample",
    )(h, h, h, h, at, conv_state, state, *params)


def _place_q(qpair, head, lane):
    half, kvh = head % 2, head // (SWA_HEADS // SWA_KV_HEADS)
    qj = jnp.where((lane >= SWA_HEAD_DIM) == (half == 1), qpair, 0.0)
    return pltpu.roll(qj, SWA_HEAD_DIM, axis=1) if half != kvh else qj


def _take_o(pv, head):
    half, kvh = head % 2, head // (SWA_HEADS // SWA_KV_HEADS)
    return pltpu.roll(pv, SWA_HEAD_DIM, axis=1) if half != kvh else pv


def _swa_finish(outs, lane, nw_ref, so_ref, rows):
    pairs = [jnp.where(lane < SWA_HEAD_DIM, outs[2 * m], outs[2 * m + 1]) for m in range(SWA_HEADS // 2)]
    ss = sum(jnp.sum(p * p, axis=-1, keepdims=True) for p in pairs)
    inv = lax.rsqrt(ss * (1.0 / SWA_WIDTH) + EPS)
    for m, p in enumerate(pairs):
        so_ref[rows, m * LANES:(m + 1) * LANES] = (p * inv * nw_ref[:, m * LANES:(m + 1) * LANES]).astype(so_ref.dtype)


def _swa_prompt_kernel(q_ref, kp_ref, kc_ref, vp_ref, vc_ref, sk_ref, nw_ref, so_ref):
    lo = jnp.where(pl.program_id(1) == 0, WINDOW, 0)
    kk = jnp.concatenate([kp_ref[...], kc_ref[...]], axis=0).astype(BF16)
    vv = jnp.concatenate([vp_ref[...], vc_ref[...]], axis=0).astype(BF16)
    lane = lax.broadcasted_iota(jnp.int32, (TILE, LANES), 1)
    qi = lax.broadcasted_iota(jnp.int32, (TILE, 2 * WINDOW), 0)
    kj = lax.broadcasted_iota(jnp.int32, (TILE, 2 * WINDOW), 1)
    diff = qi + WINDOW - kj
    allowed = (diff >= 0) & (diff < WINDOW) & (kj >= lo)
    outs = []
    for j in range(SWA_HEADS):
        m = j // 2
        qj = _place_q(q_ref[:, m * LANES:(m + 1) * LANES], j, lane).astype(BF16)
        s = jnp.where(allowed, _dot_nt(qj, kk) * (SWA_HEAD_DIM ** -0.5), NEG)
        sink = sk_ref[j:j + 1, 0:1]
        mx = jnp.maximum(jnp.max(s, axis=-1, keepdims=True), sink)
        p = jnp.exp(s - mx)
        den = jnp.sum(p, axis=-1, keepdims=True) + jnp.exp(sink - mx)
        outs.append(_take_o(_dot(p.astype(BF16), vv) / den, j))
    _swa_finish(outs, lane, nw_ref, so_ref, slice(None))


def _swa_prompt(h, sinks, nw, batch, seq):
    nt = seq // TILE
    rb = lambda b, n: b * nt + n
    rp = lambda b, n: b * nt + jnp.maximum(n - 1, 0)
    cur = lambda w, off: pl.BlockSpec((TILE, w), lambda b, n: (rb(b, n), off // w))
    prev = lambda w, off: pl.BlockSpec((TILE, w), lambda b, n: (rp(b, n), off // w))
    return pl.pallas_call(
        _swa_prompt_kernel,
        grid=(batch, nt),
        in_specs=[cur(SWA_WIDTH, H_SQ), prev(LANES, H_SK), cur(LANES, H_SK), prev(LANES, H_SV),
                  cur(LANES, H_SV), _const_spec((SWA_HEADS, LANES)), _const_spec((1, SWA_WIDTH))],
        out_specs=pl.BlockSpec((TILE, SWA_WIDTH), lambda b, n: (rb(b, n), 0)),
        out_shape=jax.ShapeDtypeStruct((batch * seq, SWA_WIDTH), BF16),
        compiler_params=pltpu.CompilerParams(dimension_semantics=("arbitrary", "arbitrary"),
                                             vmem_limit_bytes=VMEM_LIMIT),
        name="swa_prompt",
    )(h, h, h, h, h, sinks, nw)


def _swa_sample_kernel(q_ref, kn_ref, vn_ref, ck_ref, cv_ref, sk_ref, nw_ref,
                       so_ref, ok_ref, ov_ref):
    n_seq = TILE // SEQ_PAD
    rows = SWA_HEADS * SEQ_PAD
    lane8 = lax.broadcasted_iota(jnp.int32, (SEQ_PAD, LANES), 1)
    tok = (lax.broadcasted_iota(jnp.int32, (rows, LANES), 0) & (SEQ_PAD - 1)) - NEW_OFF
    col = lax.broadcasted_iota(jnp.int32, (rows, LANES), 1)
    ok_cache = (tok >= 0) & (col >= tok + 1)
    ok_new = (tok >= 0) & (col >= NEW_OFF) & (col - NEW_OFF <= tok) & (col < SEQ_PAD)
    sink = sk_ref[:, 0:1]
    rowt = lax.broadcasted_iota(jnp.int32, (WINDOW, LANES), 0)
    zpad = jnp.zeros((WINDOW - SEQ_PAD, LANES), F32)
    scale = SWA_HEAD_DIM ** -0.5

    def body(s, _):
        r = pl.multiple_of(s * SEQ_PAD, SEQ_PAD)
        q8 = q_ref[pl.ds(r, SEQ_PAD), :]
        kc, vc = ck_ref[s], cv_ref[s]
        k2 = jnp.concatenate([kn_ref[pl.ds(r, SEQ_PAD), :], zpad], axis=0)
        v2 = jnp.concatenate([vn_ref[pl.ds(r, SEQ_PAD), :], zpad], axis=0)
        keep = rowt < WINDOW - NEW_OFF
        ok_ref[s] = jnp.where(keep, pltpu.roll(kc, WINDOW - NEW_OFF, axis=0),
                              pltpu.roll(k2, WINDOW - SEQ_PAD, axis=0))
        ov_ref[s] = jnp.where(keep, pltpu.roll(vc, WINDOW - NEW_OFF, axis=0),
                              pltpu.roll(v2, WINDOW - SEQ_PAD, axis=0))
        lhs = jnp.concatenate(
            [_place_q(q8[:, (j // 2) * LANES:(j // 2 + 1) * LANES], j, lane8) for j in range(SWA_HEADS)],
            axis=0).astype(BF16)
        s1 = jnp.where(ok_cache, _dot_nt(lhs, kc.astype(BF16)) * scale, NEG)
        s2 = jnp.where(ok_new, _dot_nt(lhs, k2.astype(BF16)) * scale, NEG)
        mx = jnp.maximum(jnp.maximum(jnp.max(s1, axis=-1, keepdims=True),
                                     jnp.max(s2, axis=-1, keepdims=True)), sink)
        p1, p2 = jnp.exp(s1 - mx), jnp.exp(s2 - mx)
        den = jnp.sum(p1, axis=-1, keepdims=True) + jnp.sum(p2, axis=-1, keepdims=True) + jnp.exp(sink - mx)
        pv = (_dot(p1.astype(BF16), vc.astype(BF16)) + _dot(p2.astype(BF16), v2.astype(BF16))) / den
        outs = [_take_o(pv[j * SEQ_PAD:(j + 1) * SEQ_PAD, :], j) for j in range(SWA_HEADS)]
        _swa_finish(outs, lane8, nw_ref, so_ref, pl.ds(r, SEQ_PAD))
        return 0

    lax.fori_loop(0, n_seq, body, 0)


def _swa_sample(h, cache_k, cache_v, sinks, nw):
    n_seq = TILE // SEQ_PAD
    nt = h.shape[0] // TILE
    hspec = lambda w, off: pl.BlockSpec((TILE, w), lambda i: (i, off // w))
    cspec = pl.BlockSpec((n_seq, WINDOW, LANES), lambda i: (i, 0, 0))
    return pl.pallas_call(
        _swa_sample_kernel,
        grid=(nt,),
        in_specs=[hspec(SWA_WIDTH, H_SQ), hspec(LANES, H_SK), hspec(LANES, H_SV), cspec, cspec,
                  _const_spec((SWA_HEADS * SEQ_PAD, LANES)), _const_spec((1, SWA_WIDTH))],
        out_specs=[pl.BlockSpec((TILE, SWA_WIDTH), lambda i: (i, 0)), cspec, cspec],
        out_shape=[jax.ShapeDtypeStruct((h.shape[0], SWA_WIDTH), F32),
                   jax.ShapeDtypeStruct(cache_k.shape, F32),
                   jax.ShapeDtypeStruct(cache_v.shape, F32)],
        compiler_params=pltpu.CompilerParams(dimension_semantics=("arbitrary",),
                                             vmem_limit_bytes=VMEM_LIMIT),
        name="swa_sample",
    )(h, h, h, cache_k, cache_v, jnp.repeat(sinks, SEQ_PAD, axis=0), nw)


def _relayout_w_in(w_in):
    off_z = CONV_CH
    off_b = off_z + GDN_KEY
    off_a = off_b + GDN_HEADS
    off_sq = off_a + GDN_HEADS
    pad = jnp.zeros((w_in.shape[0], LANES - GDN_HEADS), w_in.dtype)
    cols = jnp.concatenate([w_in[:, :off_b], w_in[:, off_sq:], w_in[:, off_b:off_a], pad,
                            w_in[:, off_a:off_sq], pad], axis=1)
    wat = jnp.concatenate([w_in[:, off_a:off_sq].T,
                           jnp.zeros((SEQ_PAD - GDN_HEADS, w_in.shape[0]), w_in.dtype)], axis=0)
    return cols.astype(BF16), wat.astype(BF16)


def _head_params(a_log, dt_bias):
    padl = jnp.zeros((LANES - GDN_HEADS,), F32)
    alc = jnp.concatenate([a_log, padl])[None, :]
    dtc = jnp.concatenate([dt_bias, padl])[None, :]
    pads = jnp.zeros((SEQ_PAD - GDN_HEADS,), F32)
    alr = jnp.broadcast_to(jnp.concatenate([a_log, pads])[:, None], (SEQ_PAD, LANES))
    dtr = jnp.broadcast_to(jnp.concatenate([dt_bias, pads])[:, None], (SEQ_PAD, LANES))
    return alc, dtc, alr, dtr


def kernel(x_prompt, x_sample, state_gdn_conv, state_gdn, cache_swa_k, cache_swa_v, ffn1_norm, ffn1_w_gate, ffn1_w_up, ffn1_w_down, mix_norm, w_in, gdn_conv_w, gdn_a_log, gdn_dt_bias, gdn_out_norm, swa_sinks, swa_out_norm, w_out, ffn2_norm, ffn2_w_gate, ffn2_w_up, ffn2_w_down, final_norm):
    batch, seq, d = x_prompt.shape
    dec_batch, dec_seq, _ = x_sample.shape
    depth = w_in.shape[0]
    assert d == D_MODEL and seq % TILE == 0 and dec_seq == SEQ_PAD - NEW_OFF
    assert (batch * seq) % DENSE_TM == 0 and (dec_batch * SEQ_PAD) % DENSE_TM == 0
    assert cache_swa_k.shape[2] == WINDOW and state_gdn_conv.shape[2] == CONV_WIDTH - 1

    xp = x_prompt.reshape(batch * seq, d)
    xs = jnp.pad(x_sample, ((0, 0), (NEW_OFF, 0), (0, 0))).reshape(dec_batch * SEQ_PAD, d)
    fn = final_norm[None, :]

    pc, ps, pk, pv = [], [], [], []
    sc, ss, sk, sv = [], [], [], []
    for l in range(depth):
        win, wat = _relayout_w_in(w_in[l])
        g1, u1, d1 = ffn1_w_gate[l].astype(BF16), ffn1_w_up[l].astype(BF16), ffn1_w_down[l].astype(BF16)
        g2, u2, d2 = ffn2_w_gate[l].astype(BF16), ffn2_w_up[l].astype(BF16), ffn2_w_down[l].astype(BF16)
        wog, wos = w_out[l, :GDN_KEY].astype(BF16), w_out[l, GDN_KEY:].astype(BF16)
        n1, nm, n2 = ffn1_norm[l][None, :], mix_norm[l][None, :], ffn2_norm[l][None, :]
        gparams = (gdn_conv_w[l],) + _head_params(gdn_a_log[l], gdn_dt_bias[l]) + (gdn_out_norm[l][None, :],)
        sinks = jnp.broadcast_to(swa_sinks[l][:, None], (SWA_HEADS, LANES))
        snw = swa_out_norm[l][None, :]
        final = l == depth - 1

        xp, hp, atp = _dense_in(xp, n1, g1, u1, d1, nm, win, wat)
        gop, s_new = _gdn_prompt(hp, atp, gparams, batch, seq)
        sop = _swa_prompt(hp, sinks, snw, batch, seq)
        xp = _dense_out(xp, gop, sop, wog, wos, n2, g2, u2, d2, fn, final)
        hp3 = hp.reshape(batch, seq, H_COLS)
        pc.append(hp3[:, seq - (CONV_WIDTH - 1):, H_QKV:H_QKV + CONV_CH])
        ps.append(s_new)
        pk.append(hp3[:, seq - WINDOW:, H_SK:H_SK + SWA_KV_WIDTH].reshape(batch, WINDOW, SWA_KV_HEADS, SWA_HEAD_DIM))
        pv.append(hp3[:, seq - WINDOW:, H_SV:H_SV + SWA_KV_WIDTH].reshape(batch, WINDOW, SWA_KV_HEADS, SWA_HEAD_DIM))

        xs, hs, ats = _dense_in(xs, n1, g1, u1, d1, nm, win, wat)
        cs = jnp.pad(state_gdn_conv[l], ((0, 0), (1, SEQ_PAD - CONV_WIDTH), (0, 0))).reshape(dec_batch * SEQ_PAD, CONV_CH)
        gos, s_new = _gdn_sample(hs, ats, cs, state_gdn[l], gparams)
        sos, k_new, v_new = _swa_sample(hs, cache_swa_k[l].reshape(dec_batch, WINDOW, SWA_KV_WIDTH),
                                        cache_swa_v[l].reshape(dec_batch, WINDOW, SWA_KV_WIDTH), sinks, snw)
        xs = _dense_out(xs, gos, sos, wog, wos, n2, g2, u2, d2, fn, final)
        hs3 = hs.reshape(dec_batch, SEQ_PAD, H_COLS)
        sc.append(hs3[:, SEQ_PAD - (CONV_WIDTH - 1):, H_QKV:H_QKV + CONV_CH])
        ss.append(s_new)
        sk.append(k_new.reshape(cache_swa_k.shape[1:]))
        sv.append(v_new.reshape(cache_swa_v.shape[1:]))

    y_prompt = xp.reshape(batch, seq, d)
    y_sample = xs.reshape(dec_batch, SEQ_PAD, d)[:, NEW_OFF:]
    return (y_prompt, y_sample,
            jnp.stack(pc), jnp.stack(ps), jnp.stack(pk), jnp.stack(pv),
            jnp.stack(sc), jnp.stack(ss), jnp.stack(sk), jnp.stack(sv))
```
